```python
import math
import jax, jax.numpy as jnp
from jax import lax
import numpy as np

D_MODEL = 1024
BATCH = 8
SEQ = 4096
DEPTH = 2

D_MIX = D_MODEL
GROUP_W = D_MIX // 4
POOL_WINDOWS = (2, 4, 8, 16)
POOL_GROUPS = len(POOL_WINDOWS)
POOL_CH = GROUP_W // POOL_GROUPS
SB_HEADS = 4
SB_HEAD_DIM = GROUP_W // SB_HEADS
SB_BLOCK = 128
SGU_HEADS = 4
SGU_HEAD_DIM = GROUP_W // SGU_HEADS
SGU_CHUNK = 128
LRU_BLOCKS = 4
LRU_BLOCK_DIM = GROUP_W // LRU_BLOCKS
CONV_WIDTH = 4
LRU_C = 8.0
D_FF = 2816
OFF_POOL = 0
OFF_Q = OFF_POOL + GROUP_W
OFF_K = OFF_Q + GROUP_W
OFF_V = OFF_K + GROUP_W
OFF_SGU_U = OFF_V + GROUP_W
OFF_SGU_V = OFF_SGU_U + GROUP_W
OFF_LRU_X = OFF_SGU_V + GROUP_W
OFF_LRU_G = OFF_LRU_X + GROUP_W
D_IN = OFF_LRU_G + GROUP_W
EPS = 1e-6

kernel_name = "hybrid_parallel_groups_pool_sb_sgu_rglru_macaron"


def rms_norm(x, g):
    x32 = x.astype(jnp.float32)
    y = x32 * lax.rsqrt(jnp.mean(x32 * x32, axis=-1, keepdims=True) + EPS)
    return (y * g.astype(jnp.float32)).astype(x.dtype)


def swiglu_ffn(x, w_in, w_out):
    gate, up = jnp.split(x @ w_in, 2, axis=-1)
    return (jax.nn.silu(gate) * up) @ w_out


def pool_mixer(xp, w, scale):
    B, S, _ = xp.shape
    x32 = xp.astype(jnp.float32).reshape(B, S, POOL_GROUPS, POOL_CH)
    cs = jnp.cumsum(x32, axis=1)
    counts_base = jnp.arange(1, S + 1, dtype=jnp.int32)
    means = []
    for g, win in enumerate(POOL_WINDOWS):
        csg = cs[:, :, g]
        prev = jnp.pad(csg, ((0, 0), (win, 0), (0, 0)))[:, :S]
        cnt = jnp.minimum(counts_base, win).astype(jnp.float32)[None, :, None]
        means.append((csg - prev) / cnt)
    d = jnp.stack(means, axis=2) - x32
    y = jnp.einsum('bsgc,gcd->bsgd', d, w.astype(jnp.float32)).reshape(B, S, GROUP_W)
    return y * scale.astype(jnp.float32)


def stick_breaking_attention(q, k, v):
    B, S, _ = q.shape
    to_heads = lambda a: a.astype(jnp.float32).reshape(B, S, SB_HEADS, SB_HEAD_DIM).transpose(0, 2, 1, 3)
    q32, k32, v32 = to_heads(q), to_heads(k), to_heads(v)
    scale = 1.0 / math.sqrt(SB_HEAD_DIM)
    key_pos = jnp.arange(S)
    n_blocks = S // SB_BLOCK

    def block(i):
        start = i * SB_BLOCK
        qb = lax.dynamic_slice_in_dim(q32, start, SB_BLOCK, axis=2)
        z = jnp.einsum('bhqd,bhkd->bhqk', qb, k32) * scale
        q_pos = start + jnp.arange(SB_BLOCK)
        mask = key_pos[None, :] < q_pos[:, None]
        log_keep = jnp.where(mask, jax.nn.log_sigmoid(-z), 0.0)
        suffix_incl = jnp.flip(jnp.cumsum(jnp.flip(log_keep, -1), axis=-1), -1)
        suffix_excl = suffix_incl - log_keep
        a = jnp.where(mask, jnp.exp(jax.nn.log_sigmoid(z) + suffix_excl), 0.0)
        return jnp.einsum('bhqk,bhkd->bhqd', a, v32)

    out = lax.map(block, jnp.arange(n_blocks))
    return out.transpose(1, 0, 3, 2, 4).reshape(B, S, GROUP_W)


def spatial_gating(u, v, w_s, b_s):
    B, S, _ = u.shape
    n_chunks = S // SGU_CHUNK
    u32 = jax.nn.gelu(u.astype(jnp.float32))
    v32 = jax.nn.gelu(v.astype(jnp.float32)).reshape(B, n_chunks, SGU_CHUNK, SGU_HEADS, SGU_HEAD_DIM)
    mu = jnp.mean(v32, axis=-1, keepdims=True)
    var = jnp.mean(jnp.square(v32 - mu), axis=-1, keepdims=True)
    vn = (v32 - mu) * lax.rsqrt(var + EPS)
    tri = jnp.tril(jnp.ones((SGU_CHUNK, SGU_CHUNK), jnp.float32))
    ws = w_s.astype(jnp.float32) * tri[None]
    mixed = jnp.einsum('hts,bnshc->bnthc', ws, vn) + b_s.astype(jnp.float32).T[None, None, :, :, None]
    return u32 * mixed.reshape(B, S, GROUP_W)


def rglru_mixer(xb, gb, conv_w, conv_b, wa, ba, wx, bx, lam):
    B, S, C = xb.shape
    xc = lax.conv_general_dilated(
        xb, conv_w[:, None, :], window_strides=(1,), padding=[(CONV_WIDTH - 1, 0)],
        dimension_numbers=('NWC', 'WIO', 'NWC'), feature_group_count=C) + conv_b
    xc32 = xc.astype(jnp.float32)
    xblk = xc32.reshape(B, S, LRU_BLOCKS, LRU_BLOCK_DIM)
    r = jax.nn.sigmoid(jnp.einsum('bsgc,gcd->bsgd', xblk, wa.astype(jnp.float32)).reshape(B, S, C)
                       + ba.astype(jnp.float32))
    i = jax.nn.sigmoid(jnp.einsum('bsgc,gcd->bsgd', xblk, wx.astype(jnp.float32)).reshape(B, S, C)
                       + bx.astype(jnp.float32))
    log_a = -LRU_C * r * jax.nn.softplus(-lam.astype(jnp.float32))
    a = jnp.exp(log_a)
    mult = jnp.sqrt(-jnp.expm1(2.0 * log_a))
    b_in = mult * (i * xc32)

    def combine(e1, e2):
        a1, b1 = e1
        a2, b2 = e2
        return a1 * a2, a2 * b1 + b2

    _, h = lax.associative_scan(combine, (a, b_in), axis=1)
    return h * jax.nn.gelu(gb.astype(jnp.float32))


def setup_inputs(seed: int = 0) -> dict:
    key = jax.random.key(seed)
    ks = iter(jax.random.split(key, 32))
    f32 = jnp.float32

    def nrm(shape, scale):
        return jax.random.normal(next(ks), shape, f32) * scale

    x = jax.random.normal(next(ks), (BATCH, SEQ, D_MODEL), f32)
    ffn1_norm = 1.0 + nrm((DEPTH, D_MODEL), 0.02)
    ffn1_w_in = nrm((DEPTH, D_MODEL, 2 * D_FF), D_MODEL ** -0.5)
    ffn1_w_out = nrm((DEPTH, D_FF, D_MODEL), D_FF ** -0.5)
    mix_norm = 1.0 + nrm((DEPTH, D_MODEL), 0.02)
    mix_w_in = nrm((DEPTH, D_MODEL, D_IN), D_MODEL ** -0.5)
    mix_w_out = nrm((DEPTH, D_MIX, D_MODEL), D_MIX ** -0.5)
    pool_w = nrm((DEPTH, POOL_GROUPS, POOL_CH, POOL_CH), POOL_CH ** -0.5)
    pool_scale = 1.0 + nrm((DEPTH, GROUP_W), 0.1)
    sgu_w = nrm((DEPTH, SGU_HEADS, SGU_CHUNK, SGU_CHUNK), SGU_CHUNK ** -0.5)
    sgu_b = 1.0 + nrm((DEPTH, SGU_HEADS, SGU_CHUNK), 0.02)
    conv_w = nrm((DEPTH, CONV_WIDTH, GROUP_W), CONV_WIDTH ** -0.5)
    conv_b = nrm((DEPTH, GROUP_W), 0.01)
    lru_wa = nrm((DEPTH, LRU_BLOCKS, LRU_BLOCK_DIM, LRU_BLOCK_DIM), LRU_BLOCK_DIM ** -0.5)
    lru_ba = nrm((DEPTH, GROUP_W), 0.01)
    lru_wx = nrm((DEPTH, LRU_BLOCKS, LRU_BLOCK_DIM, LRU_BLOCK_DIM), LRU_BLOCK_DIM ** -0.5)
    lru_bx = nrm((DEPTH, GROUP_W), 0.01)
    a_c = jax.random.uniform(next(ks), (DEPTH, GROUP_W), f32, 0.9, 0.999)
    s = a_c ** (1.0 / LRU_C)
    lru_lambda = jnp.log(s) - jnp.log1p(-s)
    ffn2_norm = 1.0 + nrm((DEPTH, D_MODEL), 0.02)
    ffn2_w_in = nrm((DEPTH, D_MODEL, 2 * D_FF), D_MODEL ** -0.5)
    ffn2_w_out = nrm((DEPTH, D_FF, D_MODEL), D_FF ** -0.5)
    final_norm = 1.0 + nrm((D_MODEL,), 0.02)
    return {
        "x": x, "ffn1_norm": ffn1_norm, "ffn1_w_in": ffn1_w_in, "ffn1_w_out": ffn1_w_out,
        "mix_norm": mix_norm, "mix_w_in": mix_w_in, "mix_w_out": mix_w_out,
        "pool_w": pool_w, "pool_scale": pool_scale, "sgu_w": sgu_w, "sgu_b": sgu_b,
        "conv_w": conv_w, "conv_b": conv_b, "lru_wa": lru_wa, "lru_ba": lru_ba,
        "lru_wx": lru_wx, "lru_bx": lru_bx, "lru_lambda": lru_lambda,
        "ffn2_norm": ffn2_norm, "ffn2_w_in": ffn2_w_in, "ffn2_w_out": ffn2_w_out,
        "final_norm": final_norm,
    }


def reference(x, ffn1_norm, ffn1_w_in, ffn1_w_out, mix_norm, mix_w_in, mix_w_out,
              pool_w, pool_scale, sgu_w, sgu_b, conv_w, conv_b, lru_wa, lru_ba,
              lru_wx, lru_bx, lru_lambda, ffn2_norm, ffn2_w_in, ffn2_w_out, final_norm):
    for l in range(DEPTH):
        x = x + 0.5 * swiglu_ffn(rms_norm(x, ffn1_norm[l]), ffn1_w_in[l], ffn1_w_out[l])
        h = rms_norm(x, mix_norm[l])
        p = h @ mix_w_in[l]
        y_pool = pool_mixer(p[..., OFF_POOL:OFF_Q], pool_w[l], pool_scale[l])
        y_sb = stick_breaking_attention(p[..., OFF_Q:OFF_K], p[..., OFF_K:OFF_V], p[..., OFF_V:OFF_SGU_U])
        y_sgu = spatial_gating(p[..., OFF_SGU_U:OFF_SGU_V], p[..., OFF_SGU_V:OFF_LRU_X], sgu_w[l], sgu_b[l])
        y_lru = rglru_mixer(p[..., OFF_LRU_X:OFF_LRU_G], p[..., OFF_LRU_G:D_IN], conv_w[l], conv_b[l],
                            lru_wa[l], lru_ba[l], lru_wx[l], lru_bx[l], lru_lambda[l])
        y = jnp.concatenate([y_pool, y_sb, y_sgu, y_lru], axis=-1).astype(x.dtype)
        x = x + y @ mix_w_out[l]
        x = x + 0.5 * swiglu_ffn(rms_norm(x, ffn2_norm[l]), ffn2_w_in[l], ffn2_w_out[l])
    return rms_norm(x, final_norm)
```

```python
import functools
import math

import jax
import jax.numpy as jnp
from jax import lax
from jax.experimental import pallas as pl
from jax.experimental.pallas import tpu as pltpu

F32 = jnp.float32
BF16 = jnp.bfloat16

D_MODEL = 1024
D_FF = 2816
GROUP_W = 256
N_HEADS = 4
HEAD_DIM = 64
HEAD_SHIFT = 6
POOL_WINDOWS = (2, 4, 8, 16)
POOL_HALO = 16
SGU_CHUNK = 128
CONV_WIDTH = 4
CONV_HALO = 8
LRU_C = 8.0
EPS = 1e-6

FF_CHUNK = 256
N_FF_CHUNKS = D_FF // FF_CHUNK
TM_FFN = 512
TS_PROJ = 512
TS_MIX = 512
TQ = 256
TK = 256

VMEM_LIMIT = 56 * 1024 * 1024


def _rms_norm(x, g):
    return x * lax.rsqrt(jnp.mean(x * x, axis=-1, keepdims=True) + EPS) * g


def _dot(a, b):
    return jnp.dot(a, b, preferred_element_type=F32)


def _resident(shape):
    return pl.BlockSpec(shape, lambda *_: (0,) * len(shape), pipeline_mode=pl.Buffered(1))


def _ffn_body(x_ref, g_ref, wg_ref, wu_ref, wo_ref, gf_ref, o_ref, xn_ref, acc_ref, *, final_norm):
    x = x_ref[...]
    xn_ref[...] = _rms_norm(x, g_ref[...]).astype(BF16)
    acc_ref[...] = jnp.zeros_like(acc_ref)

    def chunk(c, carry):
        xn = xn_ref[...]
        gate = _dot(xn, wg_ref[c])
        up = _dot(xn, wu_ref[c])
        act = (jax.nn.silu(gate) * up).astype(BF16)
        acc_ref[...] += _dot(act, wo_ref[c])
        return carry

    lax.fori_loop(0, N_FF_CHUNKS, chunk, 0)
    y = x + 0.5 * acc_ref[...]
    if final_norm:
        y = _rms_norm(y, gf_ref[...])
    o_ref[...] = y


def _ffn(x2d, g, wg, wu, wo, gf, final_norm):
    n = x2d.shape[0]
    row = pl.BlockSpec((TM_FFN, D_MODEL), lambda i: (i, 0))
    return pl.pallas_call(
        functools.partial(_ffn_body, final_norm=final_norm),
        grid=(n // TM_FFN,),
        in_specs=[row, _resident((1, D_MODEL)),
                  _resident((N_FF_CHUNKS, D_MODEL, FF_CHUNK)),
                  _resident((N_FF_CHUNKS, D_MODEL, FF_CHUNK)),
                  _resident((N_FF_CHUNKS, FF_CHUNK, D_MODEL)),
                  _resident((1, D_MODEL))],
        out_specs=row,
        out_shape=jax.ShapeDtypeStruct((n, D_MODEL), F32),
        scratch_shapes=[pltpu.VMEM((TM_FFN, D_MODEL), BF16), pltpu.VMEM((TM_FFN, D_MODEL), F32)],
        compiler_params=pltpu.CompilerParams(dimension_semantics=("arbitrary",),
                                             vmem_limit_bytes=VMEM_LIMIT),
        name="ffn",
    )(x2d, g, wg, wu, wo, gf)


def _proj_body(x_ref, g_ref, w_ref, wkt_ref, q_ref, kt_ref, v_ref, po_ref):
    h = _rms_norm(x_ref[...], g_ref[...]).astype(BF16)
    po_ref[:, 0:GROUP_W] = _dot(h, w_ref[:, 0:GROUP_W])
    q_ref[...] = (_dot(h, w_ref[:, GROUP_W:2 * GROUP_W]) * (1.0 / math.sqrt(HEAD_DIM))).astype(BF16)
    kt = lax.dot_general(wkt_ref[...], h, (((1,), (1,)), ((), ())), preferred_element_type=F32)
    for j in range(TS_PROJ // TK):
        kt_ref[j] = kt[:, j * TK:(j + 1) * TK].astype(BF16)
    v_ref[...] = _dot(h, w_ref[:, 3 * GROUP_W:4 * GROUP_W]).astype(BF16)
    po_ref[:, GROUP_W:5 * GROUP_W] = _dot(h, w_ref[:, 4 * GROUP_W:8 * GROUP_W])


def _proj(x, g, w_in, wkt):
    b, s, _ = x.shape
    tok = lambda w: pl.BlockSpec((None, TS_PROJ, w), lambda bi, si: (bi, si, 0))
    return pl.pallas_call(
        _proj_body,
        grid=(b, s // TS_PROJ),
        in_specs=[tok(D_MODEL), _resident((1, D_MODEL)), _resident((D_MODEL, 8 * GROUP_W)),
                  _resident((GROUP_W, D_MODEL))],
        out_specs=[tok(GROUP_W),
                   pl.BlockSpec((None, TS_PROJ // TK, GROUP_W, TK), lambda bi, si: (bi, si, 0, 0)),
                   tok(GROUP_W), tok(5 * GROUP_W)],
        out_shape=[jax.ShapeDtypeStruct((b, s, GROUP_W), BF16),
                   jax.ShapeDtypeStruct((b, s // TK, GROUP_W, TK), BF16),
                   jax.ShapeDtypeStruct((b, s, GROUP_W), BF16),
                   jax.ShapeDtypeStruct((b, s, 5 * GROUP_W), F32)],
        compiler_params=pltpu.CompilerParams(dimension_semantics=("arbitrary", "arbitrary"),
                                             vmem_limit_bytes=VMEM_LIMIT),
        name="in_proj",
    )(x, g, w_in, wkt)


def _attn_body(q_ref, kt_ref, v_ref, o_ref, qs_ref, acc_ref, c_ref):
    i = pl.program_id(1)
    q = q_ref[...].astype(F32)
    lane = lax.broadcasted_iota(jnp.int32, (TQ, GROUP_W), 1)
    head_of_lane = lane >> HEAD_SHIFT
    for h in range(N_HEADS):
        qs_ref[h] = jnp.where(head_of_lane == h, q, 0.0).astype(BF16)
    acc_ref[...] = jnp.zeros_like(acc_ref)
    c_ref[...] = jnp.zeros_like(c_ref)

    row = lax.broadcasted_iota(jnp.int32, (TQ, TK), 0)
    col = lax.broadcasted_iota(jnp.int32, (TQ, TK), 1)
    u = jnp.where(row > col, 1.0, 0.0).astype(BF16)
    causal = col < row

    def tile(j, diag):
        kt = kt_ref[j]
        vb = v_ref[j]
        for h in range(N_HEADS):
            z = _dot(qs_ref[h], kt)
            t = jnp.maximum(z, 0.0) + jnp.log(1.0 + jnp.exp(-jnp.abs(z)))
            log_beta = z - t
            if diag:
                t = jnp.where(causal, t, 0.0)
            suffix = _dot(t.astype(BF16), u)
            c = c_ref[h]
            a = jnp.exp(log_beta - suffix - jnp.concatenate([c] * (TK // 128), axis=1))
            if diag:
                a = jnp.where(causal, a, 0.0)
            acc_ref[h] += _dot(a.astype(BF16), vb)
            block_total = suffix[:, 0:1] + t[:, 0:1]
            c_ref[h] = c + jnp.broadcast_to(block_total, (TQ, 128))

    tile(i, True)

    def earlier(it, carry):
        tile(i - 1 - it, False)
        return carry

    lax.fori_loop(0, i, earlier, 0)

    out = jnp.zeros((TQ, GROUP_W), F32)
    for h in range(N_HEADS):
        out = jnp.where(head_of_lane == h, acc_ref[h], out)
    o_ref[...] = out.astype(BF16)


def _attention(q, kt, v):
    b, s, _ = q.shape
    nk = s // TK
    v4 = v.reshape(b, nk, TK, GROUP_W)
    return pl.pallas_call(
        _attn_body,
        grid=(b, s // TQ),
        in_specs=[pl.BlockSpec((None, TQ, GROUP_W), lambda bi, qi: (bi, qi, 0)),
                  pl.BlockSpec((None, nk, GROUP_W, TK), lambda bi, qi: (bi, 0, 0, 0)),
                  pl.BlockSpec((None, nk, TK, GROUP_W), lambda bi, qi: (bi, 0, 0, 0))],
        out_specs=pl.BlockSpec((None, TQ, GROUP_W), lambda bi, qi: (bi, qi, 0)),
        out_shape=jax.ShapeDtypeStruct((b, s, GROUP_W), BF16),
        scratch_shapes=[pltpu.VMEM((N_HEADS, TQ, GROUP_W), BF16),
                        pltpu.VMEM((N_HEADS, TQ, GROUP_W), F32),
                        pltpu.VMEM((N_HEADS, TQ, 128), F32)],
        compiler_params=pltpu.CompilerParams(dimension_semantics=("arbitrary", "arbitrary"),
                                             vmem_limit_bytes=VMEM_LIMIT),
        name="sb_attention",
    )(q, kt, v4)


def _split_dot(x, w):
    hi = x.astype(BF16)
    lo = (x - hi.astype(F32)).astype(BF16)
    return _dot(hi, w) + _dot(lo, w)


def _mix_body(x_ref, po_ref, ysb_ref, pool_w_ref, pool_scale_ref, sgu_w_ref, sgu_b_ref,
              conv_w_ref, conv_b_ref, lru_w_ref, lru_ba_ref, lru_bx_ref, lru_lam_ref, wout_ref,
              o_ref, pool_prev_ref, conv_prev_ref, h_prev_ref):
    si = pl.program_id(1)
    ts = TS_MIX

    @pl.when(si == 0)
    def _():
        pool_prev_ref[...] = jnp.zeros_like(pool_prev_ref)
        conv_prev_ref[...] = jnp.zeros_like(conv_prev_ref)
        h_prev_ref[...] = jnp.zeros_like(h_prev_ref)

    lane = lax.broadcasted_iota(jnp.int32, (ts, GROUP_W), 1)
    group_of_lane = lane >> HEAD_SHIFT

    xp = po_ref[:, 0:GROUP_W]
    ext = jnp.concatenate([pool_prev_ref[...], xp], axis=0)
    pool_prev_ref[...] = xp[ts - POOL_HALO:, :]
    s2 = ext + pltpu.roll(ext, 1, 0)
    s4 = s2 + pltpu.roll(s2, 2, 0)
    s8 = s4 + pltpu.roll(s4, 4, 0)
    s16 = s8 + pltpu.roll(s8, 8, 0)
    sums = [a[POOL_HALO:, :] for a in (s2, s4, s8, s16)]
    pos = si * ts + lax.broadcasted_iota(jnp.int32, (ts, GROUP_W), 0)
    win_sum = sums[3]
    win = jnp.full((ts, GROUP_W), POOL_WINDOWS[3], jnp.int32)
    for g in (2, 1, 0):
        win_sum = jnp.where(group_of_lane == g, sums[g], win_sum)
        win = jnp.where(group_of_lane == g, POOL_WINDOWS[g], win)
    cnt = jnp.minimum(pos + 1, win).astype(F32)
    d = win_sum / cnt - xp
    y_pool = _dot(d.astype(BF16), pool_w_ref[...]) * pool_scale_ref[...]

    avg = jnp.where(lax.broadcasted_iota(jnp.int32, (GROUP_W, GROUP_W), 0) >> HEAD_SHIFT
                    == lax.broadcasted_iota(jnp.int32, (GROUP_W, GROUP_W), 1) >> HEAD_SHIFT,
                    1.0 / HEAD_DIM, 0.0).astype(BF16)
    u_act = jax.nn.gelu(po_ref[:, GROUP_W:2 * GROUP_W])
    v_act = jax.nn.gelu(po_ref[:, 2 * GROUP_W:3 * GROUP_W])
    mu = _split_dot(v_act, avg)
    vc = v_act - mu
    var = _split_dot(vc * vc, avg)
    vn = vc * lax.rsqrt(var + EPS)
    t_idx = lax.broadcasted_iota(jnp.int32, (SGU_CHUNK, N_HEADS * SGU_CHUNK), 0)
    s_idx = lax.broadcasted_iota(jnp.int32, (SGU_CHUNK, N_HEADS * SGU_CHUNK), 1) & (SGU_CHUNK - 1)
    ws = jnp.where(s_idx <= t_idx, sgu_w_ref[...], 0.0).astype(BF16)
    lane_c = lax.broadcasted_iota(jnp.int32, (SGU_CHUNK, GROUP_W), 1)
    mixed = []
    for c in range(ts // SGU_CHUNK):
        vn_c = vn[c * SGU_CHUNK:(c + 1) * SGU_CHUNK, :]
        stacked = jnp.concatenate(
            [jnp.where(lane_c >> HEAD_SHIFT == h, vn_c, 0.0) for h in range(N_HEADS)], axis=0)
        mixed.append(_dot(ws, stacked.astype(BF16)) + sgu_b_ref[...])
    y_sgu = u_act * jnp.concatenate(mixed, axis=0)

    xb = po_ref[:, 3 * GROUP_W:4 * GROUP_W]
    cext = jnp.concatenate([conv_prev_ref[...], xb], axis=0)
    conv_prev_ref[...] = xb[ts - CONV_HALO:, :]
    xc = cext * conv_w_ref[CONV_WIDTH - 1:CONV_WIDTH, :]
    for k in range(1, CONV_WIDTH):
        xc = xc + pltpu.roll(cext, k, 0) * conv_w_ref[CONV_WIDTH - 1 - k:CONV_WIDTH - k, :]
    xc = xc[CONV_HALO:, :] + conv_b_ref[...]
    gates = _dot(xc.astype(BF16), lru_w_ref[...])
    r = jax.nn.sigmoid(gates[:, 0:GROUP_W] + lru_ba_ref[...])
    ig = jax.nn.sigmoid(gates[:, GROUP_W:2 * GROUP_W] + lru_bx_ref[...])
    nlam = -lru_lam_ref[...]
    softplus_nlam = jnp.maximum(nlam, 0.0) + jnp.log1p(jnp.exp(-jnp.abs(nlam)))
    log_a = -LRU_C * r * softplus_nlam
    a = jnp.exp(log_a)
    mult = jnp.sqrt(-jnp.tanh(log_a) * (a * a + 1.0))
    b_in = mult * (ig * xc)
    rows = lax.broadcasted_iota(jnp.int32, (ts, GROUP_W), 0)
    a_cum, b_cum = a, b_in
    shift = 1
    while shift < ts:
        a_sh = jnp.where(rows < shift, 1.0, pltpu.roll(a_cum, shift, 0))
        b_sh = jnp.where(rows < shift, 0.0, pltpu.roll(b_cum, shift, 0))
        b_cum = a_cum * b_sh + b_cum
        a_cum = a_cum * a_sh
        shift *= 2
    hs = b_cum + a_cum * h_prev_ref[0:1, :]
    h_prev_ref[...] = jnp.broadcast_to(hs[ts - 1:ts, :], h_prev_ref.shape)
    y_lru = hs * jax.nn.gelu(po_ref[:, 4 * GROUP_W:5 * GROUP_W])

    y = jnp.concatenate([y_pool.astype(BF16), ysb_ref[...], y_sgu.astype(BF16), y_lru.astype(BF16)],
                        axis=1)
    o_ref[...] = x_ref[...] + _dot(y, wout_ref[...])


def _mix(x, po, ysb, pool_w, pool_scale, sgu_w, sgu_b, conv_w, conv_b, lru_w, lru_ba, lru_bx,
         lru_lam, wout):
    b, s, _ = x.shape
    tok = lambda w: pl.BlockSpec((None, TS_MIX, w), lambda bi, si: (bi, si, 0))
    return pl.pallas_call(
        _mix_body,
        grid=(b, s // TS_MIX),
        in_specs=[tok(D_MODEL), tok(5 * GROUP_W), tok(GROUP_W),
                  _resident((GROUP_W, GROUP_W)), _resident((1, GROUP_W)),
                  _resident((SGU_CHUNK, N_HEADS * SGU_CHUNK)), _resident((SGU_CHUNK, GROUP_W)),
                  _resident((CONV_WIDTH, GROUP_W)), _resident((1, GROUP_W)),
                  _resident((GROUP_W, 2 * GROUP_W)), _resident((1, GROUP_W)), _resident((1, GROUP_W)),
                  _resident((1, GROUP_W)), _resident((D_MODEL, D_MODEL))],
        out_specs=tok(D_MODEL),
        out_shape=jax.ShapeDtypeStruct((b, s, D_MODEL), F32),
        scratch_shapes=[pltpu.VMEM((POOL_HALO, GROUP_W), F32), pltpu.VMEM((CONV_HALO, GROUP_W), F32),
                        pltpu.VMEM((8, GROUP_W), F32)],
        compiler_params=pltpu.CompilerParams(dimension_semantics=("arbitrary", "arbitrary"),
                                             vmem_limit_bytes=VMEM_LIMIT),
        name="mixers_out_proj",
    )(x, po, ysb, pool_w, pool_scale, sgu_w, sgu_b, conv_w, conv_b, lru_w, lru_ba, lru_bx, lru_lam,
      wout)


def _block_diag(w):
    g, c, _ = w.shape
    eye = jnp.eye(g, dtype=w.dtype)
    return jnp.einsum("gcd,gh->gchd", w, eye).reshape(g * c, g * c)


def _ffn_weights(w_in, w_out):
    wg = w_in[:, :D_FF].reshape(D_MODEL, N_FF_CHUNKS, FF_CHUNK).transpose(1, 0, 2).astype(BF16)
    wu = w_in[:, D_FF:].reshape(D_MODEL, N_FF_CHUNKS, FF_CHUNK).transpose(1, 0, 2).astype(BF16)
    wo = w_out.reshape(N_FF_CHUNKS, FF_CHUNK, D_MODEL).astype(BF16)
    return wg, wu, wo


def kernel(x, ffn1_norm, ffn1_w_in, ffn1_w_out, mix_norm, mix_w_in, mix_w_out, pool_w, pool_scale, sgu_w, sgu_b, conv_w, conv_b, lru_wa, lru_ba, lru_wx, lru_bx, lru_lambda, ffn2_norm, ffn2_w_in, ffn2_w_out, final_norm):
    b, s, d = x.shape
    depth = ffn1_norm.shape[0]
    row = lambda v: v.reshape(1, -1).astype(F32)
    gf = row(final_norm)
    for l in range(depth):
        x = _ffn(x.reshape(b * s, d), row(ffn1_norm[l]), *_ffn_weights(ffn1_w_in[l], ffn1_w_out[l]),
                 gf, False).reshape(b, s, d)
        w_in = mix_w_in[l].astype(BF16)
        wkt = mix_w_in[l][:, 2 * GROUP_W:3 * GROUP_W].T.astype(BF16)
        q, kt, v, po = _proj(x, row(mix_norm[l]), w_in, wkt)
        ysb = _attention(q, kt, v)
        sgu_wcat = sgu_w[l].transpose(1, 0, 2).reshape(SGU_CHUNK, N_HEADS * SGU_CHUNK).astype(F32)
        sgu_bias = jnp.repeat(sgu_b[l].T, HEAD_DIM, axis=1).astype(F32)
        lru_w = jnp.concatenate([_block_diag(lru_wa[l]), _block_diag(lru_wx[l])], axis=1).astype(BF16)
        x = _mix(x, po, ysb, _block_diag(pool_w[l]).astype(BF16), row(pool_scale[l]), sgu_wcat, sgu_bias,
                 conv_w[l].astype(F32), row(conv_b[l]), lru_w, row(lru_ba[l]), row(lru_bx[l]),
                 row(lru_lambda[l]), mix_w_out[l].astype(BF16))
        x = _ffn(x.reshape(b * s, d), row(ffn2_norm[l]), *_ffn_weights(ffn2_w_in[l], ffn2_w_out[l]),
                 gf, l == depth - 1).reshape(b, s, d)
    return x
```
